```python
import jax, jax.numpy as jnp
from jax import lax
import numpy as np

D_MODEL = 1024
BATCH = 8
SEQ = 4096
DEPTH = 2

CHUNK = 64
QBLOCK = 128
EPS = 1e-6
N_BRANCH = 4

A_HEADS = 8
A_NOPE = 64
A_ROPE = 32
A_VDIM = 64
A_QRANK = D_MODEL // 4
A_KVRANK = D_MODEL // 8
A_WIDTH = A_HEADS * A_VDIM
ROPE_THETA = 10000.0

B_HEADS = 4
B_QK = 64
B_V = 128
B_WIDTH = B_HEADS * B_V
MLSTM_CHUNK = CHUNK
F_BIAS_INIT = 3.0

C_WIDTH = D_MODEL // 2
CONV_K = 3

D_GROUPS = 4
D_POS = 128
D_WIDTH = D_MODEL // 2
D_GCH = D_WIDTH // D_GROUPS

IN_SIZES = (A_QRANK, A_KVRANK, A_ROPE, A_WIDTH,
            B_HEADS * B_QK, B_HEADS * B_QK, B_WIDTH, B_HEADS, B_HEADS, B_WIDTH, B_WIDTH,
            C_WIDTH, C_WIDTH, C_WIDTH, C_WIDTH,
            2 * D_WIDTH, D_WIDTH)
IN_DIM = (A_QRANK + A_KVRANK + A_ROPE + A_WIDTH
          + 2 * B_HEADS * B_QK + 3 * B_WIDTH + 2 * B_HEADS
          + 4 * C_WIDTH + 3 * D_WIDTH)

kernel_name = 'hybrid_gated_parallel_mixers'


def split_cols(z):
    idx = []
    acc = 0
    for s in IN_SIZES[:-1]:
        acc += s
        idx.append(acc)
    return jnp.split(z, idx, axis=-1)


def rmsnorm(x, g):
    xf = x.astype(jnp.float32)
    y = xf * lax.rsqrt(jnp.mean(xf * xf, axis=-1, keepdims=True) + EPS)
    return (y * g.astype(jnp.float32)).astype(x.dtype)


def layernorm(x, g):
    xf = x.astype(jnp.float32)
    xc = xf - jnp.mean(xf, axis=-1, keepdims=True)
    y = xc * lax.rsqrt(jnp.mean(xc * xc, axis=-1, keepdims=True) + EPS)
    return (y * g.astype(jnp.float32)).astype(x.dtype)


def rope(t, pos):
    half = t.shape[-1] // 2
    inv = ROPE_THETA ** (-jnp.arange(half, dtype=jnp.float32) / half)
    ang = pos.astype(jnp.float32)[..., None] * inv
    cos = jnp.cos(ang).astype(t.dtype)
    sin = jnp.sin(ang).astype(t.dtype)
    t1, t2 = t[..., :half], t[..., half:]
    return jnp.concatenate([t1 * cos - t2 * sin, t1 * sin + t2 * cos], axis=-1)


def mla_branch(cq, ckv, krope, pos, g_cq, g_ckv, w_uq, w_ukv):
    bsz, s_len, _ = cq.shape
    q = (rmsnorm(cq, g_cq) @ w_uq).reshape(bsz, s_len, A_HEADS, A_NOPE + A_ROPE).transpose(0, 2, 1, 3)
    kv = (rmsnorm(ckv, g_ckv) @ w_ukv).reshape(bsz, s_len, A_HEADS, A_NOPE + A_VDIM).transpose(0, 2, 1, 3)
    k_nope, v = kv[..., :A_NOPE], kv[..., A_NOPE:]
    pos_h = pos[:, None, :]
    q_rope = rope(q[..., A_NOPE:], pos_h)
    k_rope = rope(krope[:, None], pos_h)
    q = jnp.concatenate([q[..., :A_NOPE], q_rope], axis=-1)
    k = jnp.concatenate([k_nope, jnp.broadcast_to(k_rope, k_nope.shape[:-1] + (A_ROPE,))], axis=-1)
    scale = (A_NOPE + A_ROPE) ** -0.5
    chunk_id = jnp.arange(s_len) // CHUNK
    outs = []
    for qb in range(s_len // QBLOCK):
        lo, hi = qb * QBLOCK, (qb + 1) * QBLOCK
        sc = jnp.einsum('bhqd,bhkd->bhqk', q[:, :, lo:hi], k[:, :, :hi]).astype(jnp.float32) * scale
        mask = chunk_id[None, :hi] <= chunk_id[lo:hi, None]
        sc = jnp.where(mask, sc, -1e30)
        p = jax.nn.softmax(sc, axis=-1).astype(v.dtype)
        outs.append(jnp.einsum('bhqk,bhkd->bhqd', p, v[:, :, :hi]))
    o = jnp.concatenate(outs, axis=2)
    return o.transpose(0, 2, 1, 3).reshape(bsz, s_len, A_WIDTH)


def mlstm_branch(q, k, v, i_pre, f_pre, o_pre, f_bias, g_mh):
    bsz, s_len, _ = q.shape
    L = MLSTM_CHUNK
    nc = s_len // L
    f32 = jnp.float32

    def heads(t, d):
        return t.reshape(bsz, nc, L, B_HEADS, d).transpose(0, 3, 1, 2, 4).astype(f32)

    def gates(t):
        return t.reshape(bsz, nc, L, B_HEADS).transpose(0, 3, 1, 2).astype(f32)

    qh = heads(q, B_QK) * (B_QK ** -0.5)
    kh = heads(k, B_QK)
    vh = heads(v, B_V)
    ig = gates(i_pre)
    lf = jax.nn.log_sigmoid(gates(f_pre + f_bias))
    b = jnp.cumsum(lf, axis=-1)
    g = b[..., -1]
    a = g[..., None] - b + ig
    m_loc = jnp.max(a, axis=-1)
    w = jnp.exp(a - m_loc[..., None])
    ck = jnp.einsum('bhcl,bhcld,bhcle->bhcde', w, kh, vh)
    nk = jnp.einsum('bhcl,bhcld->bhcd', w, kh)

    def step(carry, inp):
        c_st, n_st, m_st = carry
        ck_c, nk_c, g_c, ml_c = inp
        m_new = jnp.maximum(g_c + m_st, ml_c)
        s_old = jnp.exp(g_c + m_st - m_new)
        s_new = jnp.exp(ml_c - m_new)
        c_new = s_old[..., None, None] * c_st + s_new[..., None, None] * ck_c
        n_new = s_old[..., None] * n_st + s_new[..., None] * nk_c
        return (c_new, n_new, m_new), (c_st, n_st, m_st)

    init = (jnp.zeros((bsz, B_HEADS, B_QK, B_V), f32),
            jnp.zeros((bsz, B_HEADS, B_QK), f32),
            jnp.zeros((bsz, B_HEADS), f32))
    xs = (jnp.moveaxis(ck, 2, 0), jnp.moveaxis(nk, 2, 0), jnp.moveaxis(g, 2, 0), jnp.moveaxis(m_loc, 2, 0))
    _, (c_prev, n_prev, m_prev) = lax.scan(step, init, xs)
    c_prev = jnp.moveaxis(c_prev, 0, 2)
    n_prev = jnp.moveaxis(n_prev, 0, 2)
    m_prev = jnp.moveaxis(m_prev, 0, 2)

    causal = jnp.tril(jnp.ones((L, L), dtype=bool))
    dmat = jnp.where(causal, b[..., :, None] - b[..., None, :] + ig[..., None, :], -jnp.inf)
    inter = b + m_prev[..., None]
    m_t = jnp.maximum(jnp.max(dmat, axis=-1), inter)
    pq = jnp.exp(dmat - m_t[..., None]) * jnp.einsum('bhctd,bhcsd->bhcts', qh, kh)
    e_inter = jnp.exp(inter - m_t)
    num = jnp.einsum('bhcts,bhcse->bhcte', pq, vh) + e_inter[..., None] * jnp.einsum('bhctd,bhcde->bhcte', qh, c_prev)
    den = jnp.sum(pq, axis=-1) + e_inter * jnp.einsum('bhctd,bhcd->bhct', qh, n_prev)
    h = num / jnp.maximum(jnp.abs(den), jnp.exp(-m_t))[..., None]
    h = h * lax.rsqrt(jnp.mean(h * h, axis=-1, keepdims=True) + EPS)
    h = h.transpose(0, 2, 3, 1, 4).reshape(bsz, s_len, B_WIDTH) * g_mh.astype(f32)
    return (h * jax.nn.sigmoid(o_pre.astype(f32))).astype(v.dtype)


def shortconv_branch(xc, bg, cg, w_conv, b_conv):
    u = cg * xc
    y = lax.conv_general_dilated(u, w_conv, window_strides=(1,), padding=[(CONV_K - 1, 0)],
                                 dimension_numbers=('NWC', 'WIO', 'NWC'),
                                 feature_group_count=C_WIDTH)
    return bg * (y + b_conv)


def sgu_branch(uv, g_sv, w_s, b_s):
    bsz, s_len, _ = uv.shape
    uv = jax.nn.gelu(uv)
    u, v = uv[..., :D_WIDTH], uv[..., D_WIDTH:]
    v = layernorm(v, g_sv).reshape(bsz, s_len // D_POS, D_POS, D_GROUPS, D_GCH)
    ws = jnp.tril(w_s)
    mixed = jnp.einsum('gts,bnsgc->bntgc', ws, v) + b_s.T[None, None, :, :, None]
    return u * mixed.reshape(bsz, s_len, D_WIDTH)


def hybrid_layer(x, pos, g_pre, g_post, w_in, w_gate, b_gate, g_cq, g_ckv, w_uq, w_ukv,
                 f_bias, g_mh, w_conv, b_conv, g_sv, w_s, b_s, w_pa, w_pb, w_pc, w_pd, w_out):
    bsz, s_len, _ = x.shape
    h = rmsnorm(x, g_pre)
    (cq, ckv, krope, ga,
     mq, mk, mv, mi, mf, mo, gb,
     cx, cb, cc, gc,
     duv, gd) = split_cols(h @ w_in)
    y_a = mla_branch(cq, ckv, krope, pos, g_cq, g_ckv, w_uq, w_ukv) * jax.nn.silu(ga)
    y_b = mlstm_branch(mq, mk, mv, mi, mf, mo, f_bias, g_mh) * jax.nn.silu(gb)
    y_c = shortconv_branch(cx, cb, cc, w_conv, b_conv) * jax.nn.silu(gc)
    y_d = sgu_branch(duv, g_sv, w_s, b_s) * jax.nn.silu(gd)
    gates = jax.nn.sigmoid((h @ w_gate + b_gate).astype(jnp.float32)).astype(x.dtype)
    gates = gates.reshape(bsz, s_len, N_BRANCH, D_MODEL)
    merged = (gates[:, :, 0] * (y_a @ w_pa) + gates[:, :, 1] * (y_b @ w_pb)
              + gates[:, :, 2] * (y_c @ w_pc) + gates[:, :, 3] * (y_d @ w_pd))
    return x + rmsnorm(merged @ w_out, g_post)


def setup_inputs(seed: int = 0) -> dict:
    key = jax.random.key(seed)
    ks = jax.random.split(key, 32)
    f32 = jnp.float32

    def nrm(k, shape, fan_in):
        return jax.random.normal(k, shape, f32) * (fan_in ** -0.5)

    def gain(k, shape):
        return 1.0 + 0.05 * jax.random.normal(k, shape, f32)

    x = jax.random.normal(ks[0], (BATCH, SEQ, D_MODEL), f32)
    offsets = jax.random.randint(ks[1], (BATCH, 1), 0, 64) * CHUNK
    positions = (offsets + jnp.arange(SEQ)[None, :]).astype(jnp.int32)
    return {
        'x': x,
        'positions': positions,
        'g_pre': gain(ks[2], (DEPTH, D_MODEL)),
        'g_post': gain(ks[3], (DEPTH, D_MODEL)),
        'w_in': nrm(ks[4], (DEPTH, D_MODEL, IN_DIM), D_MODEL),
        'w_gate': nrm(ks[5], (DEPTH, D_MODEL, N_BRANCH * D_MODEL), D_MODEL),
        'b_gate': 0.01 * jax.random.normal(ks[6], (DEPTH, N_BRANCH * D_MODEL), f32),
        'g_cq': gain(ks[7], (DEPTH, A_QRANK)),
        'g_ckv': gain(ks[8], (DEPTH, A_KVRANK)),
        'w_uq': nrm(ks[9], (DEPTH, A_QRANK, A_HEADS * (A_NOPE + A_ROPE)), A_QRANK),
        'w_ukv': nrm(ks[10], (DEPTH, A_KVRANK, A_HEADS * (A_NOPE + A_VDIM)), A_KVRANK),
        'f_bias': F_BIAS_INIT + 0.1 * jax.random.normal(ks[11], (DEPTH, B_HEADS), f32),
        'g_mh': gain(ks[12], (DEPTH, B_WIDTH)),
        'w_conv': nrm(ks[13], (DEPTH, CONV_K, 1, C_WIDTH), CONV_K),
        'b_conv': 0.01 * jax.random.normal(ks[14], (DEPTH, C_WIDTH), f32),
        'g_sv': gain(ks[15], (DEPTH, D_WIDTH)),
        'w_s': nrm(ks[16], (DEPTH, D_GROUPS, D_POS, D_POS), D_POS),
        'b_s': 1.0 + 0.01 * jax.random.normal(ks[17], (DEPTH, D_GROUPS, D_POS), f32),
        'w_pa': nrm(ks[18], (DEPTH, A_WIDTH, D_MODEL), A_WIDTH),
        'w_pb': nrm(ks[19], (DEPTH, B_WIDTH, D_MODEL), B_WIDTH),
        'w_pc': nrm(ks[20], (DEPTH, C_WIDTH, D_MODEL), C_WIDTH),
        'w_pd': nrm(ks[21], (DEPTH, D_WIDTH, D_MODEL), D_WIDTH),
        'w_out': nrm(ks[22], (DEPTH, D_MODEL, D_MODEL), D_MODEL),
    }


def reference(x, positions, g_pre, g_post, w_in, w_gate, b_gate, g_cq, g_ckv, w_uq, w_ukv,
              f_bias, g_mh, w_conv, b_conv, g_sv, w_s, b_s, w_pa, w_pb, w_pc, w_pd, w_out):
    for l in range(DEPTH):
        x = hybrid_layer(x, positions, g_pre[l], g_post[l], w_in[l], w_gate[l], b_gate[l],
                         g_cq[l], g_ckv[l], w_uq[l], w_ukv[l], f_bias[l], g_mh[l],
                         w_conv[l], b_conv[l], g_sv[l], w_s[l], b_s[l],
                         w_pa[l], w_pb[l], w_pc[l], w_pd[l], w_out[l])
    return x
```

```python
import functools

import numpy as np
import jax
import jax.numpy as jnp
from jax import lax
from jax.experimental import pallas as pl
from jax.experimental.pallas import tpu as pltpu

F32 = jnp.float32
BF16 = jnp.bfloat16

D_MODEL = 1024
DEPTH = 2
CHUNK = 64
EPS = 1e-6
N_BRANCH = 4

A_HEADS = 8
A_NOPE = 64
A_ROPE = 32
A_VDIM = 64
A_QRANK = 256
A_KVRANK = 128
A_WIDTH = A_HEADS * A_VDIM
ROPE_THETA = 10000.0
ROPE_HALF = A_ROPE // 2

B_HEADS = 4
B_QK = 64
B_V = 128
B_WIDTH = B_HEADS * B_V

C_WIDTH = 512
D_GROUPS = 4
D_POS = 128
D_WIDTH = 512

LANE = 128
HEAD_SLAB = LANE
V7X_VMEM_BYTES = 64 * 1024 * 1024
VMEM_LIMIT = V7X_VMEM_BYTES - 8 * 1024 * 1024

TM_IN = 512
TM_OUT = 512
TQ = 512
TK = 512
T_ML = 256

_SIZES = dict(cq=256, ckv=128, kr=32, ga=512, mq=256, mk=256, mv=512, mi=4, mf=4, mo=512, gb=512,
              cx=512, cb=512, cc=512, gc=512, duv=1024, gd=512)
_OFF = {}
_acc = 0
for _k, _v in _SIZES.items():
    _OFF[_k] = _acc
    _acc += _v
IN_DIM = _acc

_W1_LAYOUT = (("cq", 256), ("ckv", 128), ("kr", 128), ("ga", 512), ("mq", 256), ("mv", 512),
              ("mo", 512), ("gb", 512), ("gate", 128), ("cx", 512), ("cb", 512), ("cc", 512),
              ("gc", 512), ("duv", 1024), ("gd", 512))
_W1 = {}
_acc = 0
for _k, _v in _W1_LAYOUT:
    _W1[_k] = (_acc, _acc + _v)
    _acc += _v
W1_COLS = _acc


def _w1_gather_index():
    idx = np.full((W1_COLS,), -1, np.int32)
    for name, _ in _W1_LAYOUT:
        lo, _hi = _W1[name]
        if name == "kr":
            idx[lo + A_NOPE: lo + A_NOPE + A_ROPE] = _OFF["kr"] + np.arange(A_ROPE)
        elif name == "gate":
            idx[lo: lo + B_HEADS] = _OFF["mi"] + np.arange(B_HEADS)
            idx[lo + B_HEADS: lo + 2 * B_HEADS] = _OFF["mf"] + np.arange(B_HEADS)
        else:
            idx[lo: lo + _SIZES[name]] = _OFF[name] + np.arange(_SIZES[name])
    return idx


def _sigmoid(x):
    return 1.0 / (1.0 + jnp.exp(-x))


def _silu(x):
    return x * _sigmoid(x)


def _rms(x, g):
    return x * lax.rsqrt(jnp.mean(x * x, axis=-1, keepdims=True) + EPS) * g


def _gelu_tanh(x):
    return x * (0.5 * (1.0 + jnp.tanh(0.7978845608028654 * (x + 0.044715 * (x * x * x)))))


def _log_sigmoid(x):
    return jnp.minimum(x, 0.0) - jnp.log(1.0 + jnp.exp(-jnp.abs(x)))


def _dot(a, b):
    return jnp.dot(a, b, preferred_element_type=F32)


def _dot_nt(a, b):
    return lax.dot_general(a, b, (((1,), (1,)), ((), ())), preferred_element_type=F32)


def _split3(x):
    hi = x.astype(BF16)
    r1 = x - hi.astype(F32)
    mid = r1.astype(BF16)
    lo = (r1 - mid.astype(F32)).astype(BF16)
    return hi, mid, lo


def _const_spec(shape):
    nd = len(shape)
    return pl.BlockSpec(shape, lambda *_: (0,) * nd, pipeline_mode=pl.Buffered(1))


def _inproj_body(x_ref, pos_ref, inv_ref, gpre_ref, w_ref, wkT_ref, wgT_ref, gcq_ref, gckv_ref,
                 wuq_ref, wuk_ref, wvT_ref, gmh_ref, wconv_ref, bconv_ref, gsv_ref, ws_ref, bsf_ref,
                 q_ref, k_ref, vT_ref, sga_ref, mq_ref, mkT_ref, mv_ref, mb_ref, gcol_ref, grow_ref,
                 yc_ref, yd_ref, ubuf_ref):
    tm = x_ref.shape[1]
    j = pl.program_id(1)

    def wcols(name):
        lo, hi = _W1[name]
        return w_ref[:, lo:hi]

    h = _rms(x_ref[0], gpre_ref[...]).astype(BF16)

    lo, _ = _W1["cq"]
    za = _dot(h, w_ref[:, lo:lo + 512])
    cqn = _rms(za[:, 0:A_QRANK], gcq_ref[...]).astype(BF16)
    ckvn = _rms(za[:, A_QRANK:A_QRANK + A_KVRANK], gckv_ref[...]).astype(BF16)
    kr = za[:, A_QRANK + A_KVRANK:]

    ang = pos_ref[0].astype(F32) * inv_ref[...]
    cos = jnp.cos(ang)
    sin = jnp.sin(ang)
    lane = lax.broadcasted_iota(jnp.int32, (1, LANE), 1)
    first_half = (lane >= A_NOPE) & (lane < A_NOPE + ROPE_HALF)
    second_half = (lane >= A_NOPE + ROPE_HALF) & (lane < A_NOPE + A_ROPE)
    sin_lo = jnp.where(first_half, -sin, 0.0)
    sin_hi = jnp.where(second_half, sin, 0.0)

    def rope(t):
        return (t * cos + pltpu.roll(t, LANE - ROPE_HALF, 1) * sin_lo
                + pltpu.roll(t, ROPE_HALF, 1) * sin_hi)

    scale = (A_NOPE + A_ROPE) ** -0.5
    q = _dot(cqn, wuq_ref[...]) * scale
    kn = _dot(ckvn, wuk_ref[...])
    krr = rope(kr)
    for hh in range(A_HEADS):
        sl = slice(hh * HEAD_SLAB, (hh + 1) * HEAD_SLAB)
        q_ref[0, :, sl] = rope(q[:, sl]).astype(BF16)
        k_ref[0, :, sl] = (kn[:, sl] + krr).astype(BF16)
    vT_ref[0] = _dot_nt(wvT_ref[...], ckvn).astype(BF16)
    sga_ref[0] = _silu(_dot(h, wcols("ga"))).astype(BF16)

    mq_ref[0] = (_dot(h, wcols("mq")) * (B_QK ** -0.5)).astype(BF16)
    mkT_ref[0] = _dot_nt(wkT_ref[...], h).astype(BF16)
    mv_ref[0] = _dot(h, wcols("mv")).astype(BF16)
    mo = _dot(h, wcols("mo"))
    gb = _dot(h, wcols("gb"))
    mb_ref[0] = (gmh_ref[...] * _sigmoid(mo) * _silu(gb)).astype(BF16)
    gcol_ref[0] = _dot(h, wcols("gate"))
    grow_ref[0] = _dot_nt(wgT_ref[...], h)

    u = _dot(h, wcols("cc")) * _dot(h, wcols("cx"))

    @pl.when(j == 0)
    def _():
        ubuf_ref[0:8, :] = jnp.zeros((8, C_WIDTH), F32)

    ubuf_ref[8:8 + tm, :] = u
    u1 = ubuf_ref[7:7 + tm, :]
    u2 = ubuf_ref[6:6 + tm, :]
    y = wconv_ref[0:1, :] * u2 + wconv_ref[1:2, :] * u1 + wconv_ref[2:3, :] * u
    ubuf_ref[0:8, :] = ubuf_ref[tm:tm + 8, :]
    cb = _dot(h, wcols("cb"))
    gc = _dot(h, wcols("gc"))
    yc_ref[0] = (cb * (y + bconv_ref[...]) * _silu(gc)).astype(BF16)

    g = _gelu_tanh(_dot(h, wcols("duv")))
    ud = g[:, :D_WIDTH]
    vd = g[:, D_WIDTH:]
    vc = vd - jnp.mean(vd, axis=-1, keepdims=True)
    vln = (vc * lax.rsqrt(jnp.mean(vc * vc, axis=-1, keepdims=True) + EPS) * gsv_ref[...]).astype(BF16)
    gate_d = ud * _silu(_dot(h, wcols("gd")))
    row = lax.broadcasted_iota(jnp.int32, (D_POS, D_POS), 0)
    col = lax.broadcasted_iota(jnp.int32, (D_POS, D_POS), 1)
    for gi in range(D_GROUPS):
        cs = slice(gi * LANE, (gi + 1) * LANE)
        wsg = jnp.where(row >= col, ws_ref[gi], 0.0).astype(BF16)
        for n in range(tm // D_POS):
            rs = slice(n * D_POS, (n + 1) * D_POS)
            mixed = _dot(wsg, vln[rs, cs]) + bsf_ref[:, cs]
            yd_ref[0, rs, cs] = (gate_d[rs, cs] * mixed).astype(BF16)


def _inproj(x, pos3, inv_lane, p):
    bsz, s_len, _ = x.shape
    tm = TM_IN
    grid = (bsz, s_len // tm)
    tok = lambda w: pl.BlockSpec((1, tm, w), lambda b, j: (b, j, 0))
    tokT = lambda r: pl.BlockSpec((1, r, tm), lambda b, j: (b, 0, j))
    in_specs = [
        tok(D_MODEL),
        tok(1),
        _const_spec((1, LANE)),
        _const_spec((1, D_MODEL)),
        _const_spec((D_MODEL, W1_COLS)),
        _const_spec((B_HEADS * B_QK, D_MODEL)),
        _const_spec((16, D_MODEL)),
        _const_spec((1, A_QRANK)),
        _const_spec((1, A_KVRANK)),
        _const_spec((A_QRANK, A_HEADS * HEAD_SLAB)),
        _const_spec((A_KVRANK, A_HEADS * HEAD_SLAB)),
        _const_spec((A_WIDTH, A_KVRANK)),
        _const_spec((1, B_WIDTH)),
        _const_spec((8, C_WIDTH)),
        _const_spec((1, C_WIDTH)),
        _const_spec((1, D_WIDTH)),
        _const_spec((D_GROUPS, D_POS, D_POS)),
        _const_spec((D_POS, D_WIDTH)),
    ]
    out_shape = [
        jax.ShapeDtypeStruct((bsz, s_len, A_HEADS * HEAD_SLAB), BF16),
        jax.ShapeDtypeStruct((bsz, s_len, A_HEADS * HEAD_SLAB), BF16),
        jax.ShapeDtypeStruct((bsz, A_WIDTH, s_len), BF16),
        jax.ShapeDtypeStruct((bsz, s_len, A_WIDTH), BF16),
        jax.ShapeDtypeStruct((bsz, s_len, B_HEADS * B_QK), BF16),
        jax.ShapeDtypeStruct((bsz, B_HEADS * B_QK, s_len), BF16),
        jax.ShapeDtypeStruct((bsz, s_len, B_WIDTH), BF16),
        jax.ShapeDtypeStruct((bsz, s_len, B_WIDTH), BF16),
        jax.ShapeDtypeStruct((bsz, s_len, LANE), F32),
        jax.ShapeDtypeStruct((bsz, 16, s_len), F32),
        jax.ShapeDtypeStruct((bsz, s_len, C_WIDTH), BF16),
        jax.ShapeDtypeStruct((bsz, s_len, D_WIDTH), BF16),
    ]
    out_specs = [tok(A_HEADS * HEAD_SLAB), tok(A_HEADS * HEAD_SLAB), tokT(A_WIDTH), tok(A_WIDTH),
                 tok(B_HEADS * B_QK), tokT(B_HEADS * B_QK), tok(B_WIDTH), tok(B_WIDTH), tok(LANE),
                 tokT(16), tok(C_WIDTH), tok(D_WIDTH)]
    return pl.pallas_call(
        _inproj_body,
        grid=grid,
        in_specs=in_specs,
        out_specs=out_specs,
        out_shape=out_shape,
        scratch_shapes=[pltpu.VMEM((tm + 8, C_WIDTH), F32)],
        compiler_params=pltpu.CompilerParams(
            dimension_semantics=("parallel", "arbitrary"), vmem_limit_bytes=VMEM_LIMIT),
        name="inproj",
    )(x, pos3, inv_lane, p["g_pre"], p["w1"], p["wkT"], p["wgT"], p["g_cq"], p["g_ckv"],
      p["wuq"], p["wuk"], p["wvT"], p["g_mh"], p["w_conv"], p["b_conv"], p["g_sv"], p["w_s"], p["bs_full"])


def _attn_body(q_ref, k_ref, vT_ref, sga_ref, o_ref, acc_ref, m_ref, l_ref):
    tq = q_ref.shape[1]
    tk = k_ref.shape[1]
    j = pl.program_id(1)
    i = pl.program_id(2)

    @pl.when(i == 0)
    def _():
        m_ref[...] = jnp.full(m_ref.shape, -1e30, F32)
        l_ref[...] = jnp.zeros(l_ref.shape, F32)
        acc_ref[...] = jnp.zeros(acc_ref.shape, F32)

    @pl.when(i <= j)
    def _():
        k_chunk = (i * tk + lax.broadcasted_iota(jnp.int32, (tk, tq), 0)) >> 6
        q_chunk = (j * tq + lax.broadcasted_iota(jnp.int32, (tk, tq), 1)) >> 6
        visible = k_chunk <= q_chunk
        for hh in range(A_HEADS):
            sl = slice(hh * HEAD_SLAB, (hh + 1) * HEAD_SLAB)
            vs = slice(hh * A_VDIM, (hh + 1) * A_VDIM)
            st = _dot_nt(k_ref[0, :, sl], q_ref[0, :, sl])
            st = jnp.where(visible, st, -1e30)
            m_prev = m_ref[hh:hh + 1, :]
            m_new = jnp.maximum(m_prev, jnp.max(st, axis=0, keepdims=True))
            alpha = jnp.exp(m_prev - m_new)
            p = jnp.exp(st - m_new)
            l_ref[hh:hh + 1, :] = alpha * l_ref[hh:hh + 1, :] + jnp.sum(p, axis=0, keepdims=True)
            acc_ref[vs, :] = alpha * acc_ref[vs, :] + _dot(vT_ref[0, vs, :], p.astype(BF16))
            m_ref[hh:hh + 1, :] = m_new

    @pl.when(i == j)
    def _():
        for hh in range(A_HEADS):
            vs = slice(hh * A_VDIM, (hh + 1) * A_VDIM)
            acc_ref[vs, :] = acc_ref[vs, :] / l_ref[hh:hh + 1, :]
        o_ref[0] = (acc_ref[...].T * sga_ref[0].astype(F32)).astype(BF16)


def _attention(q, k, vT, sga):
    bsz, s_len, _ = q.shape
    nq = s_len // TQ
    nk = s_len // TK
    return pl.pallas_call(
        _attn_body,
        grid=(bsz, nq, nk),
        in_specs=[
            pl.BlockSpec((1, TQ, A_HEADS * HEAD_SLAB), lambda b, j, i: (b, j, 0)),
            pl.BlockSpec((1, TK, A_HEADS * HEAD_SLAB), lambda b, j, i: (b, jnp.minimum(i, j), 0)),
            pl.BlockSpec((1, A_WIDTH, TK), lambda b, j, i: (b, 0, jnp.minimum(i, j))),
            pl.BlockSpec((1, TQ, A_WIDTH), lambda b, j, i: (b, j, 0)),
        ],
        out_specs=pl.BlockSpec((1, TQ, A_WIDTH), lambda b, j, i: (b, j, 0)),
        out_shape=jax.ShapeDtypeStruct((bsz, s_len, A_WIDTH), BF16),
        scratch_shapes=[pltpu.VMEM((A_WIDTH, TQ), F32), pltpu.VMEM((A_HEADS, TQ), F32),
                        pltpu.VMEM((A_HEADS, TQ), F32)],
        compiler_params=pltpu.CompilerParams(
            dimension_semantics=("parallel", "parallel", "arbitrary"), vmem_limit_bytes=VMEM_LIMIT),
        name="attn",
    )(q, k, vT, sga)


def _mlstm_body(q_ref, kT_ref, v_ref, mb_ref, gcol_ref, grow_ref, fbl_ref, fbs_ref, y_ref,
                cx_ref, m_ref):
    t = q_ref.shape[1]
    j = pl.program_id(1)

    @pl.when(j == 0)
    def _():
        cx_ref[...] = jnp.zeros(cx_ref.shape, F32)
        m_ref[...] = jnp.zeros(m_ref.shape, F32)

    row = lax.broadcasted_iota(jnp.int32, (t, t), 0)
    col = lax.broadcasted_iota(jnp.int32, (t, t), 1)
    causal = col <= row
    tri = jnp.where(causal, 1.0, 0.0).astype(BF16)
    triT = jnp.where(row <= col, 1.0, 0.0).astype(BF16)

    lf_col = _log_sigmoid(gcol_ref[0] + fbl_ref[...])
    lf_row = _log_sigmoid(grow_ref[0] + fbs_ref[...])
    b_col = sum(_dot(tri, part) for part in _split3(lf_col))
    b_row = sum(_dot(part, triT) for part in _split3(lf_row))
    ig_row = grow_ref[0]

    ones_col = jnp.where(lax.broadcasted_iota(jnp.int32, (t, LANE), 1) == 0, 1.0, 0.0).astype(BF16)

    for hh in range(B_HEADS):
        fi = B_HEADS + hh
        bcol = b_col[:, fi:fi + 1]
        brow = b_row[fi:fi + 1, :]
        irow = ig_row[hh:hh + 1, :]
        g_tot = brow[:, t - 1:t]
        m_prev = m_ref[hh:hh + 1, 0:1]

        qh = q_ref[0, :, hh * B_QK:(hh + 1) * B_QK]
        kTh = kT_ref[0, hh * B_QK:(hh + 1) * B_QK, :]
        vx = jnp.concatenate([v_ref[0, :, hh * B_V:(hh + 1) * B_V], ones_col], axis=1)

        dm = jnp.where(causal, bcol - brow + irow, -jnp.inf)
        inter = bcol + m_prev
        m_t = jnp.maximum(jnp.max(dm, axis=1, keepdims=True), inter)
        pq = (jnp.exp(dm - m_t) * _dot(qh, kTh)).astype(BF16)
        e_inter = jnp.exp(inter - m_t)
        cx_prev = cx_ref[hh]
        nd = _dot(pq, vx) + e_inter * _dot(qh, cx_prev.astype(BF16))
        num = nd[:, :B_V]
        den = nd[:, B_V:B_V + 1]
        hv = num / jnp.maximum(jnp.abs(den), jnp.exp(-m_t))
        hv = hv * lax.rsqrt(jnp.mean(hv * hv, axis=-1, keepdims=True) + EPS)
        y_ref[0, :, hh * B_V:(hh + 1) * B_V] = (
            hv * mb_ref[0, :, hh * B_V:(hh + 1) * B_V].astype(F32)).astype(BF16)

        a_row = g_tot - brow + irow
        m_loc = jnp.max(a_row, axis=1, keepdims=True)
        kw = (kTh.astype(F32) * jnp.exp(a_row - m_loc)).astype(BF16)
        ckx = _dot(kw, vx)
        m_new = jnp.maximum(g_tot + m_prev, m_loc)
        cx_ref[hh] = jnp.exp(g_tot + m_prev - m_new) * cx_prev + jnp.exp(m_loc - m_new) * ckx
        m_ref[hh:hh + 1, :] = jnp.broadcast_to(m_new, (1, LANE))


def _mlstm(mq, mkT, mv, mb, gcol, grow, fb_lane, fb_sub):
    bsz, s_len, _ = mq.shape
    t = T_ML
    tok = lambda w: pl.BlockSpec((1, t, w), lambda b, j: (b, j, 0))
    tokT = lambda r: pl.BlockSpec((1, r, t), lambda b, j: (b, 0, j))
    return pl.pallas_call(
        _mlstm_body,
        grid=(bsz, s_len // t),
        in_specs=[tok(B_HEADS * B_QK), tokT(B_HEADS * B_QK), tok(B_WIDTH), tok(B_WIDTH), tok(LANE),
                  tokT(16), _const_spec((1, LANE)), _const_spec((16, 1))],
        out_specs=tok(B_WIDTH),
        out_shape=jax.ShapeDtypeStruct((bsz, s_len, B_WIDTH), BF16),
        scratch_shapes=[pltpu.VMEM((B_HEADS, B_QK, 2 * B_V), F32), pltpu.VMEM((8, LANE), F32)],
        compiler_params=pltpu.CompilerParams(
            dimension_semantics=("parallel", "arbitrary"), vmem_limit_bytes=VMEM_LIMIT),
        name="mlstm",
    )(mq, mkT, mv, mb, gcol, grow, fb_lane, fb_sub)


def _outproj_body(x_ref, ya_ref, yb_ref, yc_ref, yd_ref, gpre_ref, wg_ref, bg_ref, wp_ref, wout_ref,
                  gpost_ref, o_ref):
    x = x_ref[...]
    h = _rms(x, gpre_ref[...]).astype(BF16)
    merged = None
    for bi, y_ref in enumerate((ya_ref, yb_ref, yc_ref, yd_ref)):
        cs = slice(bi * D_MODEL, (bi + 1) * D_MODEL)
        gate = _sigmoid(_dot(h, wg_ref[:, cs]) + bg_ref[:, cs])
        term = gate * _dot(y_ref[...], wp_ref[bi])
        merged = term if merged is None else merged + term
    z = _dot(merged.astype(BF16), wout_ref[...])
    o_ref[...] = x + _rms(z, gpost_ref[...])


def _outproj(x2, ya, yb, yc, yd, p):
    n_tok = x2.shape[0]
    tm = TM_OUT
    tok = lambda w: pl.BlockSpec((tm, w), lambda i: (i, 0))
    return pl.pallas_call(
        _outproj_body,
        grid=(n_tok // tm,),
        in_specs=[tok(D_MODEL), tok(A_WIDTH), tok(B_WIDTH), tok(C_WIDTH), tok(D_WIDTH),
                  _const_spec((1, D_MODEL)), _const_spec((D_MODEL, N_BRANCH * D_MODEL)),
                  _const_spec((1, N_BRANCH * D_MODEL)), _const_spec((N_BRANCH, 512, D_MODEL)),
                  _const_spec((D_MODEL, D_MODEL)), _const_spec((1, D_MODEL))],
        out_specs=tok(D_MODEL),
        out_shape=jax.ShapeDtypeStruct((n_tok, D_MODEL), F32),
        compiler_params=pltpu.CompilerParams(
            dimension_semantics=("parallel",), vmem_limit_bytes=VMEM_LIMIT),
        name="outproj",
    )(x2, ya, yb, yc, yd, p["g_pre"], p["w_gate"], p["b_gate"], p["w_p"], p["w_out"], p["g_post"])


def _prep_layer(l, g_pre, g_post, w_in, w_gate, b_gate, g_cq, g_ckv, w_uq, w_ukv, f_bias, g_mh,
                w_conv, b_conv, g_sv, w_s, b_s, w_pa, w_pb, w_pc, w_pd, w_out):
    idx = _w1_gather_index()
    wi = w_in[l]
    w1 = jnp.where(jnp.asarray(idx >= 0)[None, :], jnp.take(wi, jnp.asarray(np.maximum(idx, 0)), axis=1), 0.0)
    wkT = wi[:, _OFF["mk"]:_OFF["mk"] + 256].T
    wgT = jnp.zeros((16, D_MODEL), F32).at[0:8].set(wi[:, _OFF["mi"]:_OFF["mi"] + 8].T)
    per_q = A_NOPE + A_ROPE
    wuq = jnp.pad(w_uq[l].reshape(A_QRANK, A_HEADS, per_q), ((0, 0), (0, 0), (0, HEAD_SLAB - per_q)))
    wukv = w_ukv[l].reshape(A_KVRANK, A_HEADS, A_NOPE + A_VDIM)
    wuk = jnp.pad(wukv[:, :, :A_NOPE], ((0, 0), (0, 0), (0, HEAD_SLAB - A_NOPE)))
    wvT = wukv[:, :, A_NOPE:].reshape(A_KVRANK, A_WIDTH).T
    fb_lane = jnp.zeros((1, LANE), F32).at[0, B_HEADS:2 * B_HEADS].set(f_bias[l])
    fb_sub = jnp.zeros((16, 1), F32).at[B_HEADS:2 * B_HEADS, 0].set(f_bias[l])
    return dict(
        g_pre=g_pre[l][None, :], g_post=g_post[l][None, :],
        w1=w1.astype(BF16), wkT=wkT.astype(BF16), wgT=wgT.astype(BF16),
        g_cq=g_cq[l][None, :], g_ckv=g_ckv[l][None, :],
        wuq=wuq.reshape(A_QRANK, A_HEADS * HEAD_SLAB).astype(BF16),
        wuk=wuk.reshape(A_KVRANK, A_HEADS * HEAD_SLAB).astype(BF16),
        wvT=wvT.astype(BF16),
        fb_lane=fb_lane, fb_sub=fb_sub, g_mh=g_mh[l][None, :],
        w_conv=jnp.zeros((8, C_WIDTH), F32).at[0:3].set(w_conv[l][:, 0, :]),
        b_conv=b_conv[l][None, :], g_sv=g_sv[l][None, :], w_s=w_s[l],
        bs_full=jnp.repeat(b_s[l].T, D_WIDTH // D_GROUPS, axis=1),
        w_gate=w_gate[l].astype(BF16), b_gate=b_gate[l][None, :],
        w_p=jnp.stack([w_pa[l], w_pb[l], w_pc[l], w_pd[l]]).astype(BF16),
        w_out=w_out[l].astype(BF16),
    )


def kernel(x, positions, g_pre, g_post, w_in, w_gate, b_gate, g_cq, g_ckv, w_uq, w_ukv, f_bias, g_mh,
           w_conv, b_conv, g_sv, w_s, b_s, w_pa, w_pb, w_pc, w_pd, w_out):
    bsz, s_len, _ = x.shape
    pos3 = positions.reshape(bsz, s_len, 1)
    inv = ROPE_THETA ** (-jnp.arange(ROPE_HALF, dtype=F32) / ROPE_HALF)
    inv_lane = jnp.zeros((1, LANE), F32).at[0, A_NOPE:A_NOPE + A_ROPE].set(jnp.concatenate([inv, inv]))
    for l in range(DEPTH):
        p = _prep_layer(l, g_pre, g_post, w_in, w_gate, b_gate, g_cq, g_ckv, w_uq, w_ukv, f_bias, g_mh,
                        w_conv, b_conv, g_sv, w_s, b_s, w_pa, w_pb, w_pc, w_pd, w_out)
        q, k, vT, sga, mq, mkT, mv, mb, gcol, grow, yc, yd = _inproj(x, pos3, inv_lane, p)
        ya = _attention(q, k, vT, sga)
        yb = _mlstm(mq, mkT, mv, mb, gcol, grow, p["fb_lane"], p["fb_sub"])
        x = _outproj(x.reshape(bsz * s_len, D_MODEL), ya.reshape(bsz * s_len, A_WIDTH),
                     yb.reshape(bsz * s_len, B_WIDTH), yc.reshape(bsz * s_len, C_WIDTH),
                     yd.reshape(bsz * s_len, D_WIDTH), p).reshape(bsz, s_len, D_MODEL)
    return x
```
